```python
import jax, jax.numpy as jnp
from jax import lax
import numpy as np

D_MODEL = 1024
BATCH = 2
SEQ = 16384
DEPTH = 1
DEC_BATCH = 8
DEC_SEQ = 32
PAST_LEN = 4096

CHUNK = 64
WINDOW = 128
WIN_CHUNKS = WINDOW // CHUNK
BAND = (WIN_CHUNKS + 1) * CHUNK
HEAD_DIM = 64
ATTN_WIDTH = D_MODEL // 2
N_Q_HEADS = ATTN_WIDTH // HEAD_DIM
N_KV_HEADS = 2
GROUP = N_Q_HEADS // N_KV_HEADS
GLA_WIDTH = D_MODEL - ATTN_WIDTH
GLA_HEADS = 4
GLA_DV = GLA_WIDTH // GLA_HEADS
GLA_DK = GLA_DV // 2
GLA_LOWRANK = 16
GLA_TAU = 16.0
D_FF = 4 * D_MODEL
EPS = 1e-6

IN_SIZES = (N_Q_HEADS * HEAD_DIM,
            N_KV_HEADS * HEAD_DIM,
            N_KV_HEADS * HEAD_DIM,
            GLA_HEADS * GLA_DK,
            GLA_HEADS * GLA_DK,
            GLA_HEADS * GLA_DV,
            GLA_WIDTH,
            GLA_LOWRANK)
N_IN = sum(IN_SIZES)
IN_SPLITS = [sum(IN_SIZES[:i + 1]) for i in range(len(IN_SIZES) - 1)]

kernel_name = "hybrid_swa_sink_gla_stream_step"


def rmsnorm(x, g):
    x32 = x.astype(jnp.float32)
    ms = jnp.mean(x32 * x32, axis=-1, keepdims=True)
    return (x32 * lax.rsqrt(ms + EPS) * g.astype(jnp.float32)).astype(x.dtype)


def mixer_inputs(h, w_in, w_alpha, b_alpha, g_q, g_k):
    B, T, _ = h.shape
    z = h @ w_in
    qa, ka, va, qg, kg, vg, rg, ca = jnp.split(z, IN_SPLITS, axis=-1)
    qa = rmsnorm(qa.reshape(B, T, N_KV_HEADS, GROUP, HEAD_DIM), g_q)
    ka = rmsnorm(ka.reshape(B, T, N_KV_HEADS, HEAD_DIM), g_k)
    va = va.reshape(B, T, N_KV_HEADS, HEAD_DIM)
    qg = qg.reshape(B, T, GLA_HEADS, GLA_DK) * (GLA_DK ** -0.5)
    kg = kg.reshape(B, T, GLA_HEADS, GLA_DK)
    vg = vg.reshape(B, T, GLA_HEADS, GLA_DV)
    log_a = jax.nn.log_sigmoid((ca @ w_alpha + b_alpha).astype(jnp.float32)) / GLA_TAU
    log_a = log_a.reshape(B, T, GLA_HEADS, GLA_DK)
    return qa, ka, va, qg, kg, vg, rg, log_a


def sink_attend(q, k, v, sinks, mask):
    s = jnp.einsum('...qhgd,...khd->...hgqk', q.astype(jnp.float32), k.astype(jnp.float32)) * (HEAD_DIM ** -0.5)
    if mask is not None:
        s = jnp.where(mask, s, -jnp.inf)
    sink = sinks.astype(jnp.float32).reshape(N_KV_HEADS, GROUP)[:, :, None, None]
    m = jnp.maximum(jnp.max(s, axis=-1, keepdims=True), sink)
    p = jnp.exp(s - m)
    denom = jnp.sum(p, axis=-1, keepdims=True) + jnp.exp(sink - m)
    return jnp.einsum('...hgqk,...khd->...qhgd', p / denom, v.astype(jnp.float32))


def gla_chunked(q, k, v, log_a, s0):
    B, nc, L = q.shape[:3]
    causal = jnp.tril(jnp.ones((L, L), dtype=bool))[None, :, :, None, None]
    xs = tuple(jnp.moveaxis(a.astype(jnp.float32), 1, 0) for a in (q, k, v, log_a))

    def step(S, inp):
        qc, kc, vc, gc = inp
        b = jnp.cumsum(gc, axis=1)
        o_inter = jnp.einsum('blhk,bhkv->blhv', qc * jnp.exp(b), S)
        diff = b[:, :, None] - b[:, None, :]
        decay = jnp.exp(jnp.where(causal, diff, -jnp.inf))
        A = jnp.einsum('btshk,bshk->bhts', qc[:, :, None] * decay, kc)
        o_intra = jnp.einsum('bhts,bshv->bthv', A, vc)
        bL = b[:, -1]
        k_dec = kc * jnp.exp(bL[:, None] - b)
        S_new = jnp.exp(bL)[..., None] * S + jnp.einsum('bshk,bshv->bhkv', k_dec, vc)
        return S_new, o_inter + o_intra

    s_final, o = lax.scan(step, s0, xs)
    o = jnp.moveaxis(o, 0, 1).reshape(B, nc * L, GLA_HEADS, GLA_DV)
    return o, s_final


def mixer_output(attn_o, gla_o, rg, g_gla_out, w_out, dtype):
    B, T = attn_o.shape[:2]
    a = attn_o.reshape(B, T, ATTN_WIDTH).astype(dtype)
    g = (rmsnorm(gla_o, g_gla_out).reshape(B, T, GLA_WIDTH) * jax.nn.silu(rg.astype(jnp.float32))).astype(dtype)
    return jnp.concatenate([a, g], axis=-1) @ w_out


def ffn(x, g, w_up, w_down):
    h = rmsnorm(x, g) @ w_up
    return jnp.square(jax.nn.relu(h)) @ w_down


def prompt_layer(x, p):
    g_mix, w_in, w_alpha, b_alpha, g_q, g_k, sinks, g_gla_out, w_out, g_ffn, w_up, w_down = p
    B, T, _ = x.shape
    nc = T // CHUNK
    qa, ka, va, qg, kg, vg, rg, log_a = mixer_inputs(rmsnorm(x, g_mix), w_in, w_alpha, b_alpha, g_q, g_k)
    pad = WIN_CHUNKS * CHUNK
    kc = jnp.pad(ka, ((0, 0), (pad, 0), (0, 0), (0, 0))).reshape(B, nc + WIN_CHUNKS, CHUNK, N_KV_HEADS, HEAD_DIM)
    vc = jnp.pad(va, ((0, 0), (pad, 0), (0, 0), (0, 0))).reshape(B, nc + WIN_CHUNKS, CHUNK, N_KV_HEADS, HEAD_DIM)
    k_band = jnp.concatenate([kc[:, j:j + nc] for j in range(WIN_CHUNKS + 1)], axis=2)
    v_band = jnp.concatenate([vc[:, j:j + nc] for j in range(WIN_CHUNKS + 1)], axis=2)
    kpos = jnp.arange(nc)[:, None] * CHUNK + jnp.arange(BAND)[None, :] - pad
    mask = (kpos >= 0)[:, None, None, None, :]
    q_blk = qa.reshape(B, nc, CHUNK, N_KV_HEADS, GROUP, HEAD_DIM)
    attn_o = sink_attend(q_blk, k_band, v_band, sinks, mask).reshape(B, T, N_Q_HEADS, HEAD_DIM)
    to_chunks = lambda a: a.reshape(B, nc, CHUNK, *a.shape[2:])
    s0 = jnp.zeros((B, GLA_HEADS, GLA_DK, GLA_DV), jnp.float32)
    gla_o, s_final = gla_chunked(to_chunks(qg), to_chunks(kg), to_chunks(vg), to_chunks(log_a), s0)
    h = x + mixer_output(attn_o, gla_o, rg, g_gla_out, w_out, x.dtype)
    y = h + ffn(h, g_ffn, w_up, w_down)
    return y, ka[:, T - WINDOW:], va[:, T - WINDOW:], s_final


def sample_layer(x, cache_k, cache_v, state, p):
    g_mix, w_in, w_alpha, b_alpha, g_q, g_k, sinks, g_gla_out, w_out, g_ffn, w_up, w_down = p
    B, T, _ = x.shape
    qa, ka, va, qg, kg, vg, rg, log_a = mixer_inputs(rmsnorm(x, g_mix), w_in, w_alpha, b_alpha, g_q, g_k)
    k_all = jnp.concatenate([cache_k.astype(ka.dtype), ka], axis=1)
    v_all = jnp.concatenate([cache_v.astype(va.dtype), va], axis=1)
    attn_o = sink_attend(qa, k_all, v_all, sinks, None).reshape(B, T, N_Q_HEADS, HEAD_DIM)
    one = lambda a: a.reshape(B, 1, T, *a.shape[2:])
    gla_o, s_new = gla_chunked(one(qg), one(kg), one(vg), one(log_a), state.astype(jnp.float32))
    h = x + mixer_output(attn_o, gla_o, rg, g_gla_out, w_out, x.dtype)
    y = h + ffn(h, g_ffn, w_up, w_down)
    return y, ka, va, s_new


def setup_inputs(seed: int = 0) -> dict:
    key = jax.random.key(seed)
    ks = jax.random.split(key, 20)
    f32 = jnp.float32
    nrm = lambda k, shape, s=1.0: jax.random.normal(k, shape, f32) * s
    return {
        "x_prompt": nrm(ks[0], (BATCH, SEQ, D_MODEL)),
        "x_sample": nrm(ks[1], (DEC_BATCH, DEC_SEQ, D_MODEL)),
        "cache_k": nrm(ks[2], (DEPTH, DEC_BATCH, WINDOW, N_KV_HEADS, HEAD_DIM)),
        "cache_v": nrm(ks[3], (DEPTH, DEC_BATCH, WINDOW, N_KV_HEADS, HEAD_DIM)),
        "state_gla": nrm(ks[4], (DEPTH, DEC_BATCH, GLA_HEADS, GLA_DK, GLA_DV)),
        "g_mix": 1.0 + nrm(ks[5], (DEPTH, D_MODEL), 0.02),
        "w_in": nrm(ks[6], (DEPTH, D_MODEL, N_IN), D_MODEL ** -0.5),
        "w_alpha": nrm(ks[7], (DEPTH, GLA_LOWRANK, GLA_HEADS * GLA_DK), GLA_LOWRANK ** -0.5),
        "b_alpha": nrm(ks[8], (DEPTH, GLA_HEADS * GLA_DK), 0.1),
        "g_q": 1.0 + nrm(ks[9], (DEPTH, HEAD_DIM), 0.02),
        "g_k": 1.0 + nrm(ks[10], (DEPTH, HEAD_DIM), 0.02),
        "sinks": nrm(ks[11], (DEPTH, N_Q_HEADS), 0.5),
        "g_gla_out": 1.0 + nrm(ks[12], (DEPTH, GLA_DV), 0.02),
        "w_out": nrm(ks[13], (DEPTH, D_MODEL, D_MODEL), D_MODEL ** -0.5),
        "g_ffn": 1.0 + nrm(ks[14], (DEPTH, D_MODEL), 0.02),
        "w_up": nrm(ks[15], (DEPTH, D_MODEL, D_FF), D_MODEL ** -0.5),
        "w_down": nrm(ks[16], (DEPTH, D_FF, D_MODEL), D_FF ** -0.5),
    }


def reference(x_prompt, x_sample, cache_k, cache_v, state_gla, g_mix, w_in, w_alpha, b_alpha,
              g_q, g_k, sinks, g_gla_out, w_out, g_ffn, w_up, w_down):
    params = (g_mix, w_in, w_alpha, b_alpha, g_q, g_k, sinks, g_gla_out, w_out, g_ffn, w_up, w_down)
    yp, ys = x_prompt, x_sample
    kp_l, vp_l, sp_l, ks_l, vs_l, ss_l = [], [], [], [], [], []
    for l in range(DEPTH):
        p = tuple(w[l] for w in params)
        yp, kp, vp, sp = prompt_layer(yp, p)
        ys, k_s, v_s, s_s = sample_layer(ys, cache_k[l], cache_v[l], state_gla[l], p)
        kp_l.append(kp); vp_l.append(vp); sp_l.append(sp)
        ks_l.append(k_s); vs_l.append(v_s); ss_l.append(s_s)
    return (yp, ys, jnp.stack(kp_l), jnp.stack(vp_l), jnp.stack(sp_l),
            jnp.stack(ks_l), jnp.stack(vs_l), jnp.stack(ss_l))
```

```python
import functools

import jax
import jax.numpy as jnp
from jax import lax
from jax.experimental import pallas as pl
from jax.experimental.pallas import tpu as pltpu

F32 = jnp.float32
BF16 = jnp.bfloat16

D_MODEL = 1024
CHUNK = 64
WINDOW = 128
HEAD_DIM = 64
ATTN_WIDTH = 512
N_Q_HEADS = 8
N_KV_HEADS = 2
GLA_HEADS = 4
GLA_DK = 64
GLA_DV = 128
GLA_LOWRANK = 16
GLA_TAU = 16.0
D_FF = 4096
EPS = 1e-6

LANES = 128
KV_W = N_KV_HEADS * HEAD_DIM
GK_W = GLA_HEADS * GLA_DK
GV_W = GLA_HEADS * GLA_DV
C_QA, C_KA, C_VA, C_QG, C_KG, C_VG, C_RG, C_CA = 0, 512, 640, 768, 1024, 1280, 1792, 2304
N_IN_PAD = C_CA + LANES
PAIR = 2 * CHUNK
BAND = WINDOW + PAIR
NEG_INF = float("-inf")

VMEM_LIMIT = 56 * 1024 * 1024


def _nt(a, b):
    return lax.dot_general(a, b, (((1,), (1,)), ((), ())), preferred_element_type=F32)


def _mm(a, b):
    return jnp.dot(a, b, preferred_element_type=F32)


def _rms_rows(x, g):
    ms = jnp.mean(x * x, axis=-1, keepdims=True)
    return x * lax.rsqrt(ms + EPS) * g


def _group_rms(z, ones_blk, g):
    parts = []
    for c in range(z.shape[1] // LANES):
        zc = z[:, c * LANES:(c + 1) * LANES]
        ss = _mm((zc * zc).astype(BF16), ones_blk)
        parts.append(zc * lax.rsqrt(ss * (1.0 / HEAD_DIM) + EPS))
    zn = parts[0] if len(parts) == 1 else jnp.concatenate(parts, axis=1)
    return zn * g


def _log_sigmoid(x):
    return jnp.minimum(x, 0.0) - jnp.log1p(jnp.exp(-jnp.abs(x)))


def _chunk_cumsum(x, length):
    row = lax.broadcasted_iota(jnp.int32, x.shape, 0) % length
    sh = 1
    while sh < length:
        x = x + jnp.where(row >= sh, pltpu.roll(x, sh, 0), 0.0)
        sh *= 2
    return x


def _row_of_chunk(b, length, idx):
    n, w = b.shape
    b3 = b.reshape(n // length, length, w)
    return jnp.broadcast_to(b3[:, idx:idx + 1, :], b3.shape).reshape(n, w)


def _kv_variants(a):
    lane = lax.broadcasted_iota(jnp.int32, a.shape, 1)
    lo = lane < HEAD_DIM
    ar = pltpu.roll(a, HEAD_DIM, 1)
    z = jnp.zeros_like(a)
    return (jnp.where(lo, a, z).astype(BF16), jnp.where(lo, z, ar).astype(BF16),
            jnp.where(lo, ar, z).astype(BF16), jnp.where(lo, z, a).astype(BF16))


def _sink_softmax(s, sink_col):
    m = jnp.maximum(jnp.max(s, axis=-1, keepdims=True), sink_col)
    p = jnp.exp(s - m)
    den = jnp.sum(p, axis=-1, keepdims=True) + jnp.exp(sink_col - m)
    return p.astype(BF16), 1.0 / den


def _attend_pair(lhs, k_lo, k_hi, v_lo, v_hi, bias, sink_lo, sink_hi):
    p_lo, r_lo = _sink_softmax(_nt(lhs, k_lo) + bias, sink_lo)
    p_hi, r_hi = _sink_softmax(_nt(lhs, k_hi) + bias, sink_hi)
    o = _mm(p_lo, v_lo) + _mm(p_hi, v_hi)
    lane = lax.broadcasted_iota(jnp.int32, o.shape, 1)
    return o * jnp.where(lane < HEAD_DIM, r_lo, r_hi)


def _sink_cols(sink_ref, kvh, m):
    row = lax.broadcasted_iota(jnp.int32, (2 * m, 1), 0)
    top = row < m
    base = 4 * kvh
    lo = jnp.where(top, sink_ref[base + 0], sink_ref[base + 2])
    hi = jnp.where(top, sink_ref[base + 1], sink_ref[base + 3])
    return lo, hi


def _gla_operands(zq, zk, b, length, split):
    half = length // 2
    rowc = lax.broadcasted_iota(jnp.int32, b.shape, 0) % length
    r_end = _row_of_chunk(b, length, length - 1)
    e0 = jnp.exp(b)
    qs = zq * (GLA_DK ** -0.5)
    q_dec = qs * e0
    k_dec = zk * jnp.exp(r_end - b)
    zero = jnp.zeros_like(b)
    if split:
        lo_rows = rowc < half
        r_mid = _row_of_chunk(b, length, half - 1)
        q_lo = jnp.where(lo_rows, q_dec, zero)
        q_hi = jnp.where(lo_rows, zero, qs * jnp.exp(jnp.where(lo_rows, NEG_INF, b - r_mid)))
        k0 = zk * jnp.exp(jnp.where(lo_rows, -b, NEG_INF))
        k1 = zk * jnp.exp(r_mid - b)
        return q_dec, (q_lo, q_hi), (k0, k1), k_dec, e0
    return q_dec, (q_dec,), (zk * jnp.exp(-b),), k_dec, e0


def _head_masked(slabs, j):
    blk = [s[:, j * LANES:(j + 1) * LANES] for s in slabs]
    lane = lax.broadcasted_iota(jnp.int32, blk[0].shape, 1)
    lo = lane < GLA_DK
    z = jnp.zeros_like(blk[0])
    cat = lambda xs: xs[0] if len(xs) == 1 else jnp.concatenate(xs, axis=1)
    return (cat([jnp.where(lo, x, z) for x in blk]).astype(BF16),
            cat([jnp.where(lo, z, x) for x in blk]).astype(BF16))


def _gla_out_gate(gla_o, zrg, g_out):
    parts = []
    for h in range(GLA_HEADS):
        parts.append(_rms_rows(gla_o[:, h * GLA_DV:(h + 1) * GLA_DV], g_out))
    gn = jnp.concatenate(parts, axis=1)
    return gn * (zrg / (1.0 + jnp.exp(-zrg)))


def _intra_mask(length):
    r = lax.broadcasted_iota(jnp.int32, (PAIR, PAIR), 0)
    c = lax.broadcasted_iota(jnp.int32, (PAIR, PAIR), 1)
    return jnp.logical_and(r // length == c // length, c <= r)


def _state_block_mask():
    r = lax.broadcasted_iota(jnp.int32, (2 * GLA_DV, LANES), 0)
    c = lax.broadcasted_iota(jnp.int32, (2 * GLA_DV, LANES), 1)
    return jnp.logical_or(jnp.logical_and(r < GLA_DV, c < GLA_DK),
                          jnp.logical_and(r >= GLA_DV, c >= GLA_DK))


def _prompt_mixer_body(sink_ref, x_ref, gmix_ref, win_ref, wa_ref, ba_ref, gq_ref, gk_ref, ones_ref,
                       gout_ref, wout_ref,
                       h_ref, klast_ref, vlast_ref, stout_ref,
                       kv_s, st_s, n_s, qa_s, attn_s, qdec_s, lhs_s, rlo_s, rhi_s, kdec_s, e0_s,
                       vtok_s, vt_s, gla_s, *, tq):
    t = pl.program_id(1)
    first = t == 0
    n_pairs = tq // PAIR

    @pl.when(first)
    def _():
        st_s[...] = jnp.zeros_like(st_s)
        kv_s[:, 0:WINDOW, :] = jnp.zeros((8, WINDOW, KV_W), BF16)

    @pl.when(jnp.logical_not(first))
    def _():
        kv_s[:, 0:WINDOW, :] = kv_s[:, tq:tq + WINDOW, :]

    x = x_ref[...]
    n = _rms_rows(x, gmix_ref[...]).astype(BF16)
    n_s[...] = n
    ones_blk = ones_ref[...]

    qn = _group_rms(_mm(n, win_ref[:, C_QA:C_QA + ATTN_WIDTH]), ones_blk, gq_ref[...])
    qa_s[...] = (qn * (HEAD_DIM ** -0.5)).astype(BF16)
    kn = _group_rms(_mm(n, win_ref[:, C_KA:C_KA + KV_W]), ones_blk, gk_ref[...])
    zv = _mm(n, win_ref[:, C_VA:C_VA + KV_W])
    klast_ref[...] = kn[tq - WINDOW:tq, :]
    vlast_ref[...] = zv[tq - WINDOW:tq, :]
    for i, a in enumerate(_kv_variants(kn) + _kv_variants(zv)):
        kv_s[i, WINDOW:WINDOW + tq, :] = a

    r = lax.broadcasted_iota(jnp.int32, (2 * PAIR, BAND), 0)
    c = lax.broadcasted_iota(jnp.int32, (2 * PAIR, BAND), 1)
    second = (r // CHUNK) % 2 == 1
    valid = jnp.logical_or(jnp.logical_and(second, c >= CHUNK),
                           jnp.logical_and(jnp.logical_not(second), c < BAND - CHUNK))
    bias_reg = jnp.where(valid, 0.0, NEG_INF).astype(F32)
    before_start = c < WINDOW

    def attn_step(p, carry):
        rows = pl.ds(pl.multiple_of(p * PAIR, PAIR), PAIR)
        band = pl.ds(pl.multiple_of(p * PAIR, PAIR), BAND)
        no_past = jnp.where(jnp.logical_and(first, p == 0), NEG_INF, 0.0)
        bias = bias_reg + jnp.where(before_start, no_past, 0.0)
        for kvh in range(N_KV_HEADS):
            c0 = kvh * 2 * LANES
            lhs = jnp.concatenate([qa_s[rows, c0:c0 + LANES], qa_s[rows, c0 + LANES:c0 + 2 * LANES]], axis=0)
            s_lo, s_hi = _sink_cols(sink_ref, kvh, PAIR)
            o = _attend_pair(lhs, kv_s[2 * kvh, band, :], kv_s[2 * kvh + 1, band, :],
                             kv_s[4 + 2 * kvh, band, :], kv_s[5 + 2 * kvh, band, :], bias, s_lo, s_hi)
            attn_s[rows, c0:c0 + LANES] = o[0:PAIR].astype(BF16)
            attn_s[rows, c0 + LANES:c0 + 2 * LANES] = o[PAIR:2 * PAIR].astype(BF16)
        return carry

    lax.fori_loop(0, n_pairs, attn_step, 0)

    zca = _mm(n, win_ref[:, C_CA:C_CA + LANES]).astype(BF16)
    log_a = _log_sigmoid(_mm(zca, wa_ref[...]) + ba_ref[...]) * (1.0 / GLA_TAU)
    b = _chunk_cumsum(log_a, CHUNK)
    zqg = _mm(n, win_ref[:, C_QG:C_QG + GK_W])
    zkg = _mm(n, win_ref[:, C_KG:C_KG + GK_W])
    q_dec, q_slabs, k_slabs, k_dec, e0 = _gla_operands(zqg, zkg, b, CHUNK, True)
    qdec_s[...] = q_dec.astype(BF16)
    kdec_s[...] = k_dec.astype(BF16)
    e0_s[...] = e0
    for j in range(2):
        lhs_s[:, j * 2 * LANES:(j + 1) * 2 * LANES] = jnp.concatenate(
            [q_slabs[0][:, j * LANES:(j + 1) * LANES], q_slabs[1][:, j * LANES:(j + 1) * LANES]], axis=1).astype(BF16)
        r_lo, r_hi = _head_masked(k_slabs, j)
        rlo_s[:, j * 2 * LANES:(j + 1) * 2 * LANES] = r_lo
        rhi_s[:, j * 2 * LANES:(j + 1) * 2 * LANES] = r_hi
    zvg = _mm(n, win_ref[:, C_VG:C_VG + GV_W])
    vtok_s[...] = zvg.astype(BF16)
    for p in range(n_pairs):
        vt_s[p] = zvg[p * PAIR:(p + 1) * PAIR, :].T.astype(BF16)

    intra_ok = _intra_mask(CHUNK)
    blk_ok = _state_block_mask()
    tok_lane = lax.broadcasted_iota(jnp.int32, (2 * GLA_DV, PAIR), 1)

    def gla_step(p, carry):
        base = pl.multiple_of(p * PAIR, PAIR)
        rows = pl.ds(base, PAIR)
        for j in range(2):
            hp = slice(j * 2 * LANES, (j + 1) * 2 * LANES)
            kd = slice(j * LANES, (j + 1) * LANES)
            lhs = lhs_s[rows, hp]
            a_lo = jnp.where(intra_ok, _nt(lhs, rlo_s[rows, hp]), 0.0).astype(BF16)
            a_hi = jnp.where(intra_ok, _nt(lhs, rhi_s[rows, hp]), 0.0).astype(BF16)
            o_intra = jnp.concatenate(
                [_mm(a_lo, vtok_s[rows, j * 2 * GLA_DV:j * 2 * GLA_DV + GLA_DV]),
                 _mm(a_hi, vtok_s[rows, j * 2 * GLA_DV + GLA_DV:(j + 1) * 2 * GLA_DV])], axis=1)
            vt = vt_s[p, hp, :]
            k_dec_p = kdec_s[rows, kd]
            for cc in range(2):
                crow = pl.ds(base + cc * CHUNK, CHUNK)
                st = st_s[j]
                o_inter = _nt(qdec_s[crow, kd], st.astype(BF16))
                gla_s[crow, hp] = o_inter + o_intra[cc * CHUNK:(cc + 1) * CHUNK]
                e_tail = e0_s[pl.ds(pl.multiple_of(base + (cc + 1) * CHUNK - 8, 8), 8), kd]
                e_end = e_tail[7:8]
                vt_c = jnp.where((tok_lane // CHUNK) == cc, vt, jnp.zeros_like(vt))
                st_s[j] = st * e_end + jnp.where(blk_ok, _mm(vt_c, k_dec_p), 0.0)
        return carry

    lax.fori_loop(0, n_pairs, gla_step, 0)
    stout_ref[...] = st_s[...]

    zrg = _mm(n_s[...], win_ref[:, C_RG:C_RG + GV_W])
    g = _gla_out_gate(gla_s[...], zrg, gout_ref[...]).astype(BF16)
    out = _mm(attn_s[...], wout_ref[0:ATTN_WIDTH, :]) + _mm(g, wout_ref[ATTN_WIDTH:D_MODEL, :])
    h_ref[...] = x + out


def _const_spec(shape):
    nd = len(shape)
    return pl.BlockSpec(shape, lambda *_: (0,) * nd, pipeline_mode=pl.Buffered(1))


def _prompt_mixer(x, sinks, gmix, win, wa, ba, gq, gk, ones_blk, gout, wout, *, tq):
    bsz, seq, _ = x.shape
    nt_ = seq // tq
    n_pairs = tq // PAIR
    body = functools.partial(_prompt_mixer_body, tq=tq)
    out_shape = [
        jax.ShapeDtypeStruct((bsz, seq, D_MODEL), F32),
        jax.ShapeDtypeStruct((bsz, WINDOW, KV_W), F32),
        jax.ShapeDtypeStruct((bsz, WINDOW, KV_W), F32),
        jax.ShapeDtypeStruct((bsz, 2, 2 * GLA_DV, LANES), F32),
    ]
    in_specs = [
        pl.BlockSpec(memory_space=pltpu.SMEM),
        pl.BlockSpec((None, tq, D_MODEL), lambda b, t: (b, t, 0)),
        _const_spec((1, D_MODEL)),
        _const_spec((D_MODEL, N_IN_PAD)),
        _const_spec((LANES, GK_W)),
        _const_spec((1, GK_W)),
        _const_spec((1, ATTN_WIDTH)),
        _const_spec((1, KV_W)),
        _const_spec((LANES, LANES)),
        _const_spec((1, GLA_DV)),
        _const_spec((D_MODEL, D_MODEL)),
    ]
    out_specs = [
        pl.BlockSpec((None, tq, D_MODEL), lambda b, t: (b, t, 0)),
        pl.BlockSpec((None, WINDOW, KV_W), lambda b, t: (b, 0, 0)),
        pl.BlockSpec((None, WINDOW, KV_W), lambda b, t: (b, 0, 0)),
        pl.BlockSpec((None, 2, 2 * GLA_DV, LANES), lambda b, t: (b, 0, 0, 0)),
    ]
    scratch = [
        pltpu.VMEM((8, WINDOW + tq, KV_W), BF16),
        pltpu.VMEM((2, 2 * GLA_DV, LANES), F32),
        pltpu.VMEM((tq, D_MODEL), BF16),
        pltpu.VMEM((tq, ATTN_WIDTH), BF16),
        pltpu.VMEM((tq, ATTN_WIDTH), BF16),
        pltpu.VMEM((tq, GK_W), BF16),
        pltpu.VMEM((tq, 2 * GK_W), BF16),
        pltpu.VMEM((tq, 2 * GK_W), BF16),
        pltpu.VMEM((tq, 2 * GK_W), BF16),
        pltpu.VMEM((tq, GK_W), BF16),
        pltpu.VMEM((tq, GK_W), F32),
        pltpu.VMEM((tq, GV_W), BF16),
        pltpu.VMEM((n_pairs, GV_W, PAIR), BF16),
        pltpu.VMEM((tq, GV_W), F32),
    ]
    return pl.pallas_call(
        body,
        grid=(bsz, nt_),
        in_specs=in_specs,
        out_specs=out_specs,
        out_shape=out_shape,
        scratch_shapes=scratch,
        compiler_params=pltpu.CompilerParams(
            dimension_semantics=("arbitrary", "arbitrary"), vmem_limit_bytes=VMEM_LIMIT),
        name="prompt_mixer",
    )(sinks, x, gmix, win, wa, ba, gq, gk, ones_blk, gout, wout)


def _sample_mixer_body(sink_ref, x_ref, ck_ref, cv_ref, st0_ref, gmix_ref, win_ref, wa_ref, ba_ref, gq_ref,
                       gk_ref, ones_ref, gout_ref, wout_ref,
                       h_ref, knew_ref, vnew_ref, stout_ref,
                       attn_s, gla_s, *, n_streams, t_new):
    rows_all = n_streams * t_new
    x = x_ref[...]
    n = _rms_rows(x, gmix_ref[...]).astype(BF16)
    ones_blk = ones_ref[...]

    qn = _group_rms(_mm(n, win_ref[:, C_QA:C_QA + ATTN_WIDTH]), ones_blk, gq_ref[...])
    qa = (qn * (HEAD_DIM ** -0.5)).astype(BF16)
    kn = _group_rms(_mm(n, win_ref[:, C_KA:C_KA + KV_W]), ones_blk, gk_ref[...])
    zv = _mm(n, win_ref[:, C_VA:C_VA + KV_W])
    knew_ref[...] = kn
    vnew_ref[...] = zv
    new_vars = _kv_variants(kn) + _kv_variants(zv)
    old_vars = _kv_variants(ck_ref[...]) + _kv_variants(cv_ref[...])

    n_keys = WINDOW + t_new
    pad = BAND - n_keys
    col = lax.broadcasted_iota(jnp.int32, (2 * t_new, BAND), 1)
    bias = jnp.where(col < n_keys, 0.0, NEG_INF).astype(F32)
    zpad = jnp.zeros((pad, KV_W), BF16)
    for s in range(n_streams):
        kv = [jnp.concatenate([old_vars[i][s * WINDOW:(s + 1) * WINDOW], new_vars[i][s * t_new:(s + 1) * t_new], zpad],
                              axis=0) for i in range(8)]
        rows = slice(s * t_new, (s + 1) * t_new)
        for kvh in range(N_KV_HEADS):
            c0 = kvh * 2 * LANES
            lhs = jnp.concatenate([qa[rows, c0:c0 + LANES], qa[rows, c0 + LANES:c0 + 2 * LANES]], axis=0)
            s_lo, s_hi = _sink_cols(sink_ref, kvh, t_new)
            o = _attend_pair(lhs, kv[2 * kvh], kv[2 * kvh + 1], kv[4 + 2 * kvh], kv[5 + 2 * kvh], bias, s_lo, s_hi)
            attn_s[rows, c0:c0 + LANES] = o[0:t_new].astype(BF16)
            attn_s[rows, c0 + LANES:c0 + 2 * LANES] = o[t_new:2 * t_new].astype(BF16)

    zca = _mm(n, win_ref[:, C_CA:C_CA + LANES]).astype(BF16)
    log_a = _log_sigmoid(_mm(zca, wa_ref[...]) + ba_ref[...]) * (1.0 / GLA_TAU)
    b = _chunk_cumsum(log_a, t_new)
    zqg = _mm(n, win_ref[:, C_QG:C_QG + GK_W])
    zkg = _mm(n, win_ref[:, C_KG:C_KG + GK_W])
    q_dec, q_slabs, k_slabs, k_dec, e0 = _gla_operands(zqg, zkg, b, t_new, False)
    q_dec = q_dec.astype(BF16)
    k_dec = k_dec.astype(BF16)
    zvg = _mm(n, win_ref[:, C_VG:C_VG + GV_W])
    vtok = zvg.astype(BF16)
    intra_ok = _intra_mask(t_new)
    blk_ok = _state_block_mask()
    tok_lane = lax.broadcasted_iota(jnp.int32, (2 * GLA_DV, PAIR), 1)
    per_group = PAIR // t_new
    for grp in range(rows_all // PAIR):
        rows = slice(grp * PAIR, (grp + 1) * PAIR)
        vt_g = zvg[rows, :].T.astype(BF16)
        for j in range(2):
            hp = slice(j * 2 * LANES, (j + 1) * 2 * LANES)
            kd = slice(j * LANES, (j + 1) * LANES)
            r_lo, r_hi = _head_masked([k_slabs[0][rows]], j)
            lhs = q_dec[rows, kd]
            a_lo = jnp.where(intra_ok, _nt(lhs, r_lo), 0.0).astype(BF16)
            a_hi = jnp.where(intra_ok, _nt(lhs, r_hi), 0.0).astype(BF16)
            o_intra = jnp.concatenate(
                [_mm(a_lo, vtok[rows, j * 2 * GLA_DV:j * 2 * GLA_DV + GLA_DV]),
                 _mm(a_hi, vtok[rows, j * 2 * GLA_DV + GLA_DV:(j + 1) * 2 * GLA_DV])], axis=1)
            vt = vt_g[hp, :]
            for cc in range(per_group):
                s = grp * per_group + cc
                crow = slice(s * t_new, (s + 1) * t_new)
                st = st0_ref[s, j]
                o_inter = _nt(q_dec[crow, kd], st.astype(BF16))
                gla_s[crow, hp] = o_inter + o_intra[cc * t_new:(cc + 1) * t_new]
                e_end = e0[(s + 1) * t_new - 1:(s + 1) * t_new, kd]
                vt_c = jnp.where((tok_lane // t_new) == cc, vt, jnp.zeros_like(vt))
                stout_ref[s, j] = st * e_end + jnp.where(blk_ok, _mm(vt_c, k_dec[rows, kd]), 0.0)

    zrg = _mm(n, win_ref[:, C_RG:C_RG + GV_W])
    g = _gla_out_gate(gla_s[...], zrg, gout_ref[...]).astype(BF16)
    out = _mm(attn_s[...], wout_ref[0:ATTN_WIDTH, :]) + _mm(g, wout_ref[ATTN_WIDTH:D_MODEL, :])
    h_ref[...] = x + out


def _sample_mixer(x, ck, cv, st0, sinks, gmix, win, wa, ba, gq, gk, ones_blk, gout, wout, *, n_streams, t_new):
    rows = n_streams * t_new
    body = functools.partial(_sample_mixer_body, n_streams=n_streams, t_new=t_new)
    vm = pl.BlockSpec(memory_space=pltpu.VMEM)
    out_shape = [
        jax.ShapeDtypeStruct((rows, D_MODEL), F32),
        jax.ShapeDtypeStruct((rows, KV_W), F32),
        jax.ShapeDtypeStruct((rows, KV_W), F32),
        jax.ShapeDtypeStruct((n_streams, 2, 2 * GLA_DV, LANES), F32),
    ]
    return pl.pallas_call(
        body,
        in_specs=[pl.BlockSpec(memory_space=pltpu.SMEM)] + [vm] * 13,
        out_specs=[vm] * 4,
        out_shape=out_shape,
        scratch_shapes=[pltpu.VMEM((rows, ATTN_WIDTH), BF16), pltpu.VMEM((rows, GV_W), F32)],
        compiler_params=pltpu.CompilerParams(vmem_limit_bytes=VMEM_LIMIT),
        name="sample_mixer",
    )(sinks, x, ck, cv, st0, gmix, win, wa, ba, gq, gk, ones_blk, gout, wout)


def _ffn_body(h_ref, g_ref, wup_ref, wdown_ref, y_ref, *, ff_blk):
    h = h_ref[...]
    hn = _rms_rows(h, g_ref[...]).astype(BF16)
    acc = h
    for k in range(D_FF // ff_blk):
        up = _mm(hn, wup_ref[:, k * ff_blk:(k + 1) * ff_blk])
        act = jnp.square(jnp.maximum(up, 0.0)).astype(BF16)
        acc = acc + _mm(act, wdown_ref[k * ff_blk:(k + 1) * ff_blk, :])
    y_ref[...] = acc


def _ffn(h, g, wup, wdown, *, tm, ff_blk):
    rows = h.shape[0]
    return pl.pallas_call(
        functools.partial(_ffn_body, ff_blk=ff_blk),
        grid=(rows // tm,),
        in_specs=[
            pl.BlockSpec((tm, D_MODEL), lambda i: (i, 0)),
            _const_spec((1, D_MODEL)),
            _const_spec((D_MODEL, D_FF)),
            _const_spec((D_FF, D_MODEL)),
        ],
        out_specs=pl.BlockSpec((tm, D_MODEL), lambda i: (i, 0)),
        out_shape=jax.ShapeDtypeStruct((rows, D_MODEL), F32),
        compiler_params=pltpu.CompilerParams(
            dimension_semantics=("arbitrary",), vmem_limit_bytes=VMEM_LIMIT),
        name="ffn",
    )(h, g, wup, wdown)


def _state_to_pairs(s):
    st = jnp.swapaxes(s, -1, -2)
    z = jnp.zeros_like(st[:, 0])
    pairs = []
    for j in range(2):
        top = jnp.concatenate([st[:, 2 * j], z], axis=-1)
        bot = jnp.concatenate([z, st[:, 2 * j + 1]], axis=-1)
        pairs.append(jnp.concatenate([top, bot], axis=-2))
    return jnp.stack(pairs, axis=1)


def _pairs_to_state(p):
    heads = []
    for j in range(2):
        heads.append(p[:, j, 0:GLA_DV, 0:GLA_DK])
        heads.append(p[:, j, GLA_DV:2 * GLA_DV, GLA_DK:2 * GLA_DK])
    return jnp.swapaxes(jnp.stack(heads, axis=1), -1, -2)


def kernel(x_prompt, x_sample, cache_k, cache_v, state_gla, g_mix, w_in, w_alpha, b_alpha, g_q, g_k, sinks,
           g_gla_out, w_out, g_ffn, w_up, w_down):
    bsz, seq, _ = x_prompt.shape
    n_streams, t_new, _ = x_sample.shape
    tq = min(512, seq)
    win = jnp.pad(w_in[0], ((0, 0), (0, N_IN_PAD - w_in.shape[-1]))).astype(BF16)
    wa = jnp.pad(w_alpha[0], ((0, LANES - GLA_LOWRANK), (0, 0))).astype(BF16)
    ba = b_alpha[0].reshape(1, GK_W)
    gmix = g_mix[0].reshape(1, D_MODEL)
    gffn = g_ffn[0].reshape(1, D_MODEL)
    gq = jnp.tile(g_q[0], N_Q_HEADS).reshape(1, ATTN_WIDTH)
    gk = jnp.tile(g_k[0], N_KV_HEADS).reshape(1, KV_W)
    gout = g_gla_out[0].reshape(1, GLA_DV)
    wout = w_out[0].astype(BF16)
    wup = w_up[0].astype(BF16)
    wdown = w_down[0].astype(BF16)
    sk = sinks[0]
    lane = jnp.arange(LANES)
    ones_blk = (lane[:, None] // HEAD_DIM == lane[None, :] // HEAD_DIM).astype(BF16)

    h_p, k_p, v_p, st_p = _prompt_mixer(x_prompt, sk, gmix, win, wa, ba, gq, gk, ones_blk, gout, wout, tq=tq)
    y_p = _ffn(h_p.reshape(bsz * seq, D_MODEL), gffn, wup, wdown, tm=min(512, bsz * seq), ff_blk=512)

    rows = n_streams * t_new
    ck = cache_k[0].reshape(n_streams * WINDOW, KV_W)
    cv = cache_v[0].reshape(n_streams * WINDOW, KV_W)
    st0 = _state_to_pairs(state_gla[0])
    h_s, k_s, v_s, st_s = _sample_mixer(x_sample.reshape(rows, D_MODEL), ck, cv, st0, sk, gmix, win, wa, ba, gq, gk,
                                        ones_blk, gout, wout, n_streams=n_streams, t_new=t_new)
    y_s = _ffn(h_s, gffn, wup, wdown, tm=rows, ff_blk=512)

    kv_shape = (1, bsz, WINDOW, N_KV_HEADS, HEAD_DIM)
    new_shape = (1, n_streams, t_new, N_KV_HEADS, HEAD_DIM)
    return (y_p.reshape(bsz, seq, D_MODEL), y_s.reshape(n_streams, t_new, D_MODEL),
            k_p.reshape(kv_shape), v_p.reshape(kv_shape), _pairs_to_state(st_p)[None],
            k_s.reshape(new_shape), v_s.reshape(new_shape), _pairs_to_state(st_s)[None])
```

```python
import functools

import jax
import jax.numpy as jnp
from jax import lax
from jax.experimental import pallas as pl
from jax.experimental.pallas import tpu as pltpu

F32 = jnp.float32
BF16 = jnp.bfloat16

D_MODEL = 1024
CHUNK = 64
WINDOW = 128
HEAD_DIM = 64
ATTN_WIDTH = 512
N_Q_HEADS = 8
N_KV_HEADS = 2
GLA_HEADS = 4
GLA_DK = 64
GLA_DV = 128
GLA_LOWRANK = 16
GLA_TAU = 16.0
D_FF = 4096
EPS = 1e-6

LANES = 128
KV_W = N_KV_HEADS * HEAD_DIM
GK_W = GLA_HEADS * GLA_DK
GV_W = GLA_HEADS * GLA_DV
C_QA, C_KA, C_VA, C_QG, C_KG, C_VG, C_RG, C_CA = 0, 512, 640, 768, 1024, 1280, 1792, 2304
N_IN_PAD = C_CA + LANES
PAIR = 2 * CHUNK
BAND = WINDOW + PAIR
NEG_INF = float("-inf")

VMEM_LIMIT = 56 * 1024 * 1024


def _nt(a, b):
    return lax.dot_general(a, b, (((1,), (1,)), ((), ())), preferred_element_type=F32)


def _mm(a, b):
    return jnp.dot(a, b, preferred_element_type=F32)


def _rms_rows(x, g):
    ms = jnp.mean(x * x, axis=-1, keepdims=True)
    return x * lax.rsqrt(ms + EPS) * g


def _group_rms(z, ones_blk, g):
    parts = []
    for c in range(z.shape[1] // LANES):
        zc = z[:, c * LANES:(c + 1) * LANES]
        ss = _mm((zc * zc).astype(BF16), ones_blk)
        parts.append(zc * lax.rsqrt(ss * (1.0 / HEAD_DIM) + EPS))
    zn = parts[0] if len(parts) == 1 else jnp.concatenate(parts, axis=1)
    return zn * g


def _log_sigmoid(x):
    return jnp.minimum(x, 0.0) - jnp.log1p(jnp.exp(-jnp.abs(x)))


def _chunk_cumsum(x, length):
    row = lax.broadcasted_iota(jnp.int32, x.shape, 0) % length
    sh = 1
    while sh < length:
        x = x + jnp.where(row >= sh, pltpu.roll(x, sh, 0), 0.0)
        sh *= 2
    return x


def _row_of_chunk(b, length, idx):
    n, w = b.shape
    b3 = b.reshape(n // length, length, w)
    return jnp.broadcast_to(b3[:, idx:idx + 1, :], b3.shape).reshape(n, w)


def _kv_variants(a):
    lane = lax.broadcasted_iota(jnp.int32, a.shape, 1)
    lo = lane < HEAD_DIM
    ar = pltpu.roll(a, HEAD_DIM, 1)
    z = jnp.zeros_like(a)
    return (jnp.where(lo, a, z).astype(BF16), jnp.where(lo, z, ar).astype(BF16),
            jnp.where(lo, ar, z).astype(BF16), jnp.where(lo, z, a).astype(BF16))


def _sink_softmax(s, sink_col):
    m = jnp.maximum(jnp.max(s, axis=-1, keepdims=True), sink_col)
    p = jnp.exp(s - m)
    den = jnp.sum(p, axis=-1, keepdims=True) + jnp.exp(sink_col - m)
    return p.astype(BF16), 1.0 / den


def _attend_pair(lhs, k_lo, k_hi, v_lo, v_hi, bias, sink_lo, sink_hi):
    p_lo, r_lo = _sink_softmax(_nt(lhs, k_lo) + bias, sink_lo)
    p_hi, r_hi = _sink_softmax(_nt(lhs, k_hi) + bias, sink_hi)
    o = _mm(p_lo, v_lo) + _mm(p_hi, v_hi)
    lane = lax.broadcasted_iota(jnp.int32, o.shape, 1)
    return o * jnp.where(lane < HEAD_DIM, r_lo, r_hi)


def _sink_cols(sink_ref, kvh, m):
    row = lax.broadcasted_iota(jnp.int32, (2 * m, 1), 0)
    top = row < m
    base = 4 * kvh
    lo = jnp.where(top, sink_ref[base + 0], sink_ref[base + 2])
    hi = jnp.where(top, sink_ref[base + 1], sink_ref[base + 3])
    return lo, hi


def _gla_operands(zq, zk, b, length, split):
    half = length // 2
    rowc = lax.broadcasted_iota(jnp.int32, b.shape, 0) % length
    r_end = _row_of_chunk(b, length, length - 1)
    e0 = jnp.exp(b)
    qs = zq * (GLA_DK ** -0.5)
    q_dec = qs * e0
    k_dec = zk * jnp.exp(r_end - b)
    zero = jnp.zeros_like(b)
    if split:
        lo_rows = rowc < half
        r_mid = _row_of_chunk(b, length, half - 1)
        q_lo = jnp.where(lo_rows, q_dec, zero)
        q_hi = jnp.where(lo_rows, zero, qs * jnp.exp(jnp.where(lo_rows, NEG_INF, b - r_mid)))
        k0 = zk * jnp.exp(jnp.where(lo_rows, -b, NEG_INF))
        k1 = zk * jnp.exp(r_mid - b)
        return q_dec, (q_lo, q_hi), (k0, k1), k_dec, e0
    return q_dec, (q_dec,), (zk * jnp.exp(-b),), k_dec, e0


def _head_masked(slabs, j):
    blk = [s[:, j * LANES:(j + 1) * LANES] for s in slabs]
    lane = lax.broadcasted_iota(jnp.int32, blk[0].shape, 1)
    lo = lane < GLA_DK
    z = jnp.zeros_like(blk[0])
    cat = lambda xs: xs[0] if len(xs) == 1 else jnp.concatenate(xs, axis=1)
    return (cat([jnp.where(lo, x, z) for x in blk]).astype(BF16),
            cat([jnp.where(lo, z, x) for x in blk]).astype(BF16))


def _gla_out_gate(gla_o, zrg, g_out):
    parts = []
    for h in range(GLA_HEADS):
        parts.append(_rms_rows(gla_o[:, h * GLA_DV:(h + 1) * GLA_DV], g_out))
    gn = jnp.concatenate(parts, axis=1)
    return gn * (zrg / (1.0 + jnp.exp(-zrg)))


def _intra_mask(length):
    r = lax.broadcasted_iota(jnp.int32, (PAIR, PAIR), 0)
    c = lax.broadcasted_iota(jnp.int32, (PAIR, PAIR), 1)
    return jnp.logical_and(r // length == c // length, c <= r)


def _state_block_mask():
    r = lax.broadcasted_iota(jnp.int32, (2 * GLA_DV, LANES), 0)
    c = lax.broadcasted_iota(jnp.int32, (2 * GLA_DV, LANES), 1)
    return jnp.logical_or(jnp.logical_and(r < GLA_DV, c < GLA_DK),
                          jnp.logical_and(r >= GLA_DV, c >= GLA_DK))


def _prompt_mixer_body(sink_ref, x_ref, gmix_ref, win_ref, wa_ref, ba_ref, gq_ref, gk_ref, ones_ref,
                       gout_ref, wout_ref,
                       h_ref, klast_ref, vlast_ref, stout_ref,
                       kv_s, st_s, n_s, qa_s, attn_s, qdec_s, lhs_s, rlo_s, rhi_s, kdec_s, e0_s,
                       vtok_s, vt_s, gla_s, *, tq):
    t = pl.program_id(1)
    first = t == 0
    n_pairs = tq // PAIR

    @pl.when(first)
    def _():
        st_s[...] = jnp.zeros_like(st_s)
        kv_s[:, 0:WINDOW, :] = jnp.zeros((8, WINDOW, KV_W), BF16)

    @pl.when(jnp.logical_not(first))
    def _():
        kv_s[:, 0:WINDOW, :] = kv_s[:, tq:tq + WINDOW, :]

    x = x_ref[...]
    n = _rms_rows(x, gmix_ref[...]).astype(BF16)
    n_s[...] = n
    ones_blk = ones_ref[...]

    qn = _group_rms(_mm(n, win_ref[:, C_QA:C_QA + ATTN_WIDTH]), ones_blk, gq_ref[...])
    qa_s[...] = (qn * (HEAD_DIM ** -0.5)).astype(BF16)
    kn = _group_rms(_mm(n, win_ref[:, C_KA:C_KA + KV_W]), ones_blk, gk_ref[...])
    zv = _mm(n, win_ref[:, C_VA:C_VA + KV_W])
    klast_ref[...] = kn[tq - WINDOW:tq, :]
    vlast_ref[...] = zv[tq - WINDOW:tq, :]
    for i, a in enumerate(_kv_variants(kn) + _kv_variants(zv)):
        kv_s[i, WINDOW:WINDOW + tq, :] = a

    r = lax.broadcasted_iota(jnp.int32, (2 * PAIR, BAND), 0)
    c = lax.broadcasted_iota(jnp.int32, (2 * PAIR, BAND), 1)
    second = (r // CHUNK) % 2 == 1
    valid = jnp.logical_or(jnp.logical_and(second, c >= CHUNK),
                           jnp.logical_and(jnp.logical_not(second), c < BAND - CHUNK))
    bias_reg = jnp.where(valid, 0.0, NEG_INF).astype(F32)
    before_start = c < WINDOW

    def attn_step(p, carry):
        rows = pl.ds(p * PAIR, PAIR)
        band = pl.ds(p * PAIR, BAND)
        no_past = jnp.where(jnp.logical_and(first, p == 0), NEG_INF, 0.0)
        bias = bias_reg + jnp.where(before_start, no_past, 0.0)
        for kvh in range(N_KV_HEADS):
            c0 = kvh * 2 * LANES
            lhs = jnp.concatenate([qa_s[rows, c0:c0 + LANES], qa_s[rows, c0 + LANES:c0 + 2 * LANES]], axis=0)
            s_lo, s_hi = _sink_cols(sink_ref, kvh, PAIR)
            o = _attend_pair(lhs, kv_s[2 * kvh, band, :], kv_s[2 * kvh + 1, band, :],
                             kv_s[4 + 2 * kvh, band, :], kv_s[5 + 2 * kvh, band, :], bias, s_lo, s_hi)
            attn_s[rows, c0:c0 + LANES] = o[0:PAIR].astype(BF16)
            attn_s[rows, c0 + LANES:c0 + 2 * LANES] = o[PAIR:2 * PAIR].astype(BF16)
        return carry

    for p in range(n_pairs):
        attn_step(p, 0)

    zca = _mm(n, win_ref[:, C_CA:C_CA + LANES]).astype(BF16)
    log_a = _log_sigmoid(_mm(zca, wa_ref[...]) + ba_ref[...]) * (1.0 / GLA_TAU)
    b = _chunk_cumsum(log_a, CHUNK)
    zqg = _mm(n, win_ref[:, C_QG:C_QG + GK_W])
    zkg = _mm(n, win_ref[:, C_KG:C_KG + GK_W])
    q_dec, q_slabs, k_slabs, k_dec, e0 = _gla_operands(zqg, zkg, b, CHUNK, True)
    qdec_s[...] = q_dec.astype(BF16)
    kdec_s[...] = k_dec.astype(BF16)
    e0_s[...] = e0
    for j in range(2):
        lhs_s[:, j * 2 * LANES:(j + 1) * 2 * LANES] = jnp.concatenate(
            [q_slabs[0][:, j * LANES:(j + 1) * LANES], q_slabs[1][:, j * LANES:(j + 1) * LANES]], axis=1).astype(BF16)
        r_lo, r_hi = _head_masked(k_slabs, j)
        rlo_s[:, j * 2 * LANES:(j + 1) * 2 * LANES] = r_lo
        rhi_s[:, j * 2 * LANES:(j + 1) * 2 * LANES] = r_hi
    zvg = _mm(n, win_ref[:, C_VG:C_VG + GV_W])
    vtok_s[...] = zvg.astype(BF16)
    for p in range(n_pairs):
        vt_s[p] = zvg[p * PAIR:(p + 1) * PAIR, :].T.astype(BF16)

    intra_ok = _intra_mask(CHUNK)
    blk_ok = _state_block_mask()
    tok_lane = lax.broadcasted_iota(jnp.int32, (2 * GLA_DV, PAIR), 1)

    def gla_step(p, carry):
        base = p * PAIR
        rows = pl.ds(base, PAIR)
        for j in range(2):
            hp = slice(j * 2 * LANES, (j + 1) * 2 * LANES)
            kd = slice(j * LANES, (j + 1) * LANES)
            lhs = lhs_s[rows, hp]
            a_lo = jnp.where(intra_ok, _nt(lhs, rlo_s[rows, hp]), 0.0).astype(BF16)
            a_hi = jnp.where(intra_ok, _nt(lhs, rhi_s[rows, hp]), 0.0).astype(BF16)
            o_intra = jnp.concatenate(
                [_mm(a_lo, vtok_s[rows, j * 2 * GLA_DV:j * 2 * GLA_DV + GLA_DV]),
                 _mm(a_hi, vtok_s[rows, j * 2 * GLA_DV + GLA_DV:(j + 1) * 2 * GLA_DV])], axis=1)
            vt = vt_s[p, hp, :]
            k_dec_p = kdec_s[rows, kd]
            for cc in range(2):
                crow = pl.ds(base + cc * CHUNK, CHUNK)
                st = st_s[j]
                o_inter = _nt(qdec_s[crow, kd], st.astype(BF16))
                gla_s[crow, hp] = o_inter + o_intra[cc * CHUNK:(cc + 1) * CHUNK]
                e_tail = e0_s[pl.ds(base + (cc + 1) * CHUNK - 8, 8), kd]
                e_end = e_tail[7:8]
                vt_c = jnp.where((tok_lane // CHUNK) == cc, vt, jnp.zeros_like(vt))
                st_s[j] = st * e_end + jnp.where(blk_ok, _mm(vt_c, k_dec_p), 0.0)
        return carry

    for p in range(n_pairs):
        gla_step(p, 0)
    stout_ref[...] = st_s[...]

    zrg = _mm(n_s[...], win_ref[:, C_RG:C_RG + GV_W])
    g = _gla_out_gate(gla_s[...], zrg, gout_ref[...]).astype(BF16)
    out = _mm(attn_s[...], wout_ref[0:ATTN_WIDTH, :]) + _mm(g, wout_ref[ATTN_WIDTH:D_MODEL, :])
    h_ref[...] = x + out


def _const_spec(shape):
    nd = len(shape)
    return pl.BlockSpec(shape, lambda *_: (0,) * nd, pipeline_mode=pl.Buffered(1))


def _prompt_mixer(x, sinks, gmix, win, wa, ba, gq, gk, ones_blk, gout, wout, *, tq):
    bsz, seq, _ = x.shape
    nt_ = seq // tq
    n_pairs = tq // PAIR
    body = functools.partial(_prompt_mixer_body, tq=tq)
    out_shape = [
        jax.ShapeDtypeStruct((bsz, seq, D_MODEL), F32),
        jax.ShapeDtypeStruct((bsz, WINDOW, KV_W), F32),
        jax.ShapeDtypeStruct((bsz, WINDOW, KV_W), F32),
        jax.ShapeDtypeStruct((bsz, 2, 2 * GLA_DV, LANES), F32),
    ]
    in_specs = [
        pl.BlockSpec(memory_space=pltpu.SMEM),
        pl.BlockSpec((None, tq, D_MODEL), lambda b, t: (b, t, 0)),
        _const_spec((1, D_MODEL)),
        _const_spec((D_MODEL, N_IN_PAD)),
        _const_spec((LANES, GK_W)),
        _const_spec((1, GK_W)),
        _const_spec((1, ATTN_WIDTH)),
        _const_spec((1, KV_W)),
        _const_spec((LANES, LANES)),
        _const_spec((1, GLA_DV)),
        _const_spec((D_MODEL, D_MODEL)),
    ]
    out_specs = [
        pl.BlockSpec((None, tq, D_MODEL), lambda b, t: (b, t, 0)),
        pl.BlockSpec((None, WINDOW, KV_W), lambda b, t: (b, 0, 0)),
        pl.BlockSpec((None, WINDOW, KV_W), lambda b, t: (b, 0, 0)),
        pl.BlockSpec((None, 2, 2 * GLA_DV, LANES), lambda b, t: (b, 0, 0, 0)),
    ]
    scratch = [
        pltpu.VMEM((8, WINDOW + tq, KV_W), BF16),
        pltpu.VMEM((2, 2 * GLA_DV, LANES), F32),
        pltpu.VMEM((tq, D_MODEL), BF16),
        pltpu.VMEM((tq, ATTN_WIDTH), BF16),
        pltpu.VMEM((tq, ATTN_WIDTH), BF16),
        pltpu.VMEM((tq, GK_W), BF16),
        pltpu.VMEM((tq, 2 * GK_W), BF16),
        pltpu.VMEM((tq, 2 * GK_W), BF16),
        pltpu.VMEM((tq, 2 * GK_W), BF16),
        pltpu.VMEM((tq, GK_W), BF16),
        pltpu.VMEM((tq, GK_W), F32),
        pltpu.VMEM((tq, GV_W), BF16),
        pltpu.VMEM((n_pairs, GV_W, PAIR), BF16),
        pltpu.VMEM((tq, GV_W), F32),
    ]
    return pl.pallas_call(
        body,
        grid=(bsz, nt_),
        in_specs=in_specs,
        out_specs=out_specs,
        out_shape=out_shape,
        scratch_shapes=scratch,
        compiler_params=pltpu.CompilerParams(
            dimension_semantics=("arbitrary", "arbitrary"), vmem_limit_bytes=VMEM_LIMIT),
        name="prompt_mixer",
    )(sinks, x, gmix, win, wa, ba, gq, gk, ones_blk, gout, wout)


def _sample_mixer_body(sink_ref, x_ref, ck_ref, cv_ref, st0_ref, gmix_ref, win_ref, wa_ref, ba_ref, gq_ref,
                       gk_ref, ones_ref, gout_ref, wout_ref,
                       h_ref, knew_ref, vnew_ref, stout_ref,
                       attn_s, gla_s, *, n_streams, t_new):
    rows_all = n_streams * t_new
    x = x_ref[...]
    n = _rms_rows(x, gmix_ref[...]).astype(BF16)
    ones_blk = ones_ref[...]

    qn = _group_rms(_mm(n, win_ref[:, C_QA:C_QA + ATTN_WIDTH]), ones_blk, gq_ref[...])
    qa = (qn * (HEAD_DIM ** -0.5)).astype(BF16)
    kn = _group_rms(_mm(n, win_ref[:, C_KA:C_KA + KV_W]), ones_blk, gk_ref[...])
    zv = _mm(n, win_ref[:, C_VA:C_VA + KV_W])
    knew_ref[...] = kn
    vnew_ref[...] = zv
    new_vars = _kv_variants(kn) + _kv_variants(zv)
    old_vars = _kv_variants(ck_ref[...]) + _kv_variants(cv_ref[...])

    n_keys = WINDOW + t_new
    pad = BAND - n_keys
    col = lax.broadcasted_iota(jnp.int32, (2 * t_new, BAND), 1)
    bias = jnp.where(col < n_keys, 0.0, NEG_INF).astype(F32)
    zpad = jnp.zeros((pad, KV_W), BF16)
    for s in range(n_streams):
        kv = [jnp.concatenate([old_vars[i][s * WINDOW:(s + 1) * WINDOW], new_vars[i][s * t_new:(s + 1) * t_new], zpad],
                              axis=0) for i in range(8)]
        rows = slice(s * t_new, (s + 1) * t_new)
        for kvh in range(N_KV_HEADS):
            c0 = kvh * 2 * LANES
            lhs = jnp.concatenate([qa[rows, c0:c0 + LANES], qa[rows, c0 + LANES:c0 + 2 * LANES]], axis=0)
            s_lo, s_hi = _sink_cols(sink_ref, kvh, t_new)
            o = _attend_pair(lhs, kv[2 * kvh], kv[2 * kvh + 1], kv[4 + 2 * kvh], kv[5 + 2 * kvh], bias, s_lo, s_hi)
            attn_s[rows, c0:c0 + LANES] = o[0:t_new].astype(BF16)
            attn_s[rows, c0 + LANES:c0 + 2 * LANES] = o[t_new:2 * t_new].astype(BF16)

    zca = _mm(n, win_ref[:, C_CA:C_CA + LANES]).astype(BF16)
    log_a = _log_sigmoid(_mm(zca, wa_ref[...]) + ba_ref[...]) * (1.0 / GLA_TAU)
    b = _chunk_cumsum(log_a, t_new)
    zqg = _mm(n, win_ref[:, C_QG:C_QG + GK_W])
    zkg = _mm(n, win_ref[:, C_KG:C_KG + GK_W])
    q_dec, q_slabs, k_slabs, k_dec, e0 = _gla_operands(zqg, zkg, b, t_new, False)
    q_dec = q_dec.astype(BF16)
    k_dec = k_dec.astype(BF16)
    zvg = _mm(n, win_ref[:, C_VG:C_VG + GV_W])
    vtok = zvg.astype(BF16)
    intra_ok = _intra_mask(t_new)
    blk_ok = _state_block_mask()
    tok_lane = lax.broadcasted_iota(jnp.int32, (2 * GLA_DV, PAIR), 1)
    per_group = PAIR // t_new
    for grp in range(rows_all // PAIR):
        rows = slice(grp * PAIR, (grp + 1) * PAIR)
        vt_g = zvg[rows, :].T.astype(BF16)
        for j in range(2):
            hp = slice(j * 2 * LANES, (j + 1) * 2 * LANES)
            kd = slice(j * LANES, (j + 1) * LANES)
            r_lo, r_hi = _head_masked([k_slabs[0][rows]], j)
            lhs = q_dec[rows, kd]
            a_lo = jnp.where(intra_ok, _nt(lhs, r_lo), 0.0).astype(BF16)
            a_hi = jnp.where(intra_ok, _nt(lhs, r_hi), 0.0).astype(BF16)
            o_intra = jnp.concatenate(
                [_mm(a_lo, vtok[rows, j * 2 * GLA_DV:j * 2 * GLA_DV + GLA_DV]),
                 _mm(a_hi, vtok[rows, j * 2 * GLA_DV + GLA_DV:(j + 1) * 2 * GLA_DV])], axis=1)
            vt = vt_g[hp, :]
            for cc in range(per_group):
                s = grp * per_group + cc
                crow = slice(s * t_new, (s + 1) * t_new)
                st = st0_ref[s, j]
                o_inter = _nt(q_dec[crow, kd], st.astype(BF16))
                gla_s[crow, hp] = o_inter + o_intra[cc * t_new:(cc + 1) * t_new]
                e_end = e0[(s + 1) * t_new - 1:(s + 1) * t_new, kd]
                vt_c = jnp.where((tok_lane // t_new) == cc, vt, jnp.zeros_like(vt))
                stout_ref[s, j] = st * e_end + jnp.where(blk_ok, _mm(vt_c, k_dec[rows, kd]), 0.0)

    zrg = _mm(n, win_ref[:, C_RG:C_RG + GV_W])
    g = _gla_out_gate(gla_s[...], zrg, gout_ref[...]).astype(BF16)
    out = _mm(attn_s[...], wout_ref[0:ATTN_WIDTH, :]) + _mm(g, wout_ref[ATTN_WIDTH:D_MODEL, :])
    h_ref[...] = x + out


def _sample_mixer(x, ck, cv, st0, sinks, gmix, win, wa, ba, gq, gk, ones_blk, gout, wout, *, n_streams, t_new):
    rows = n_streams * t_new
    body = functools.partial(_sample_mixer_body, n_streams=n_streams, t_new=t_new)
    vm = pl.BlockSpec(memory_space=pltpu.VMEM)
    out_shape = [
        jax.ShapeDtypeStruct((rows, D_MODEL), F32),
        jax.ShapeDtypeStruct((rows, KV_W), F32),
        jax.ShapeDtypeStruct((rows, KV_W), F32),
        jax.ShapeDtypeStruct((n_streams, 2, 2 * GLA_DV, LANES), F32),
    ]
    return pl.pallas_call(
        body,
        in_specs=[pl.BlockSpec(memory_space=pltpu.SMEM)] + [vm] * 13,
        out_specs=[vm] * 4,
        out_shape=out_shape,
        scratch_shapes=[pltpu.VMEM((rows, ATTN_WIDTH), BF16), pltpu.VMEM((rows, GV_W), F32)],
        compiler_params=pltpu.CompilerParams(vmem_limit_bytes=VMEM_LIMIT),
        name="sample_mixer",
    )(sinks, x, ck, cv, st0, gmix, win, wa, ba, gq, gk, ones_blk, gout, wout)


def _ffn_body(h_ref, g_ref, wup_ref, wdown_ref, y_ref, *, ff_blk):
    h = h_ref[...]
    hn = _rms_rows(h, g_ref[...]).astype(BF16)
    acc = h
    for k in range(D_FF // ff_blk):
        up = _mm(hn, wup_ref[:, k * ff_blk:(k + 1) * ff_blk])
        act = jnp.square(jnp.maximum(up, 0.0)).astype(BF16)
        acc = acc + _mm(act, wdown_ref[k * ff_blk:(k + 1) * ff_blk, :])
    y_ref[...] = acc


def _ffn(h, g, wup, wdown, *, tm, ff_blk):
    rows = h.shape[0]
    return pl.pallas_call(
        functools.partial(_ffn_body, ff_blk=ff_blk),
        grid=(rows // tm,),
        in_specs=[
            pl.BlockSpec((tm, D_MODEL), lambda i: (i, 0)),
            _const_spec((1, D_MODEL)),
            _const_spec((D_MODEL, D_FF)),
            _const_spec((D_FF, D_MODEL)),
        ],
        out_specs=pl.BlockSpec((tm, D_MODEL), lambda i: (i, 0)),
        out_shape=jax.ShapeDtypeStruct((rows, D_MODEL), F32),
        compiler_params=pltpu.CompilerParams(
            dimension_semantics=("arbitrary",), vmem_limit_bytes=VMEM_LIMIT),
        name="ffn",
    )(h, g, wup, wdown)


def _state_to_pairs(s):
    st = jnp.swapaxes(s, -1, -2)
    z = jnp.zeros_like(st[:, 0])
    pairs = []
    for j in range(2):
        top = jnp.concatenate([st[:, 2 * j], z], axis=-1)
        bot = jnp.concatenate([z, st[:, 2 * j + 1]], axis=-1)
        pairs.append(jnp.concatenate([top, bot], axis=-2))
    return jnp.stack(pairs, axis=1)


def _pairs_to_state(p):
    heads = []
    for j in range(2):
        heads.append(p[:, j, 0:GLA_DV, 0:GLA_DK])
        heads.append(p[:, j, GLA_DV:2 * GLA_DV, GLA_DK:2 * GLA_DK])
    return jnp.swapaxes(jnp.stack(heads, axis=1), -1, -2)


def kernel(x_prompt, x_sample, cache_k, cache_v, state_gla, g_mix, w_in, w_alpha, b_alpha, g_q, g_k, sinks,
           g_gla_out, w_out, g_ffn, w_up, w_down):
    bsz, seq, _ = x_prompt.shape
    n_streams, t_new, _ = x_sample.shape
    tq = min(512, seq)
    win = jnp.pad(w_in[0], ((0, 0), (0, N_IN_PAD - w_in.shape[-1]))).astype(BF16)
    wa = jnp.pad(w_alpha[0], ((0, LANES - GLA_LOWRANK), (0, 0))).astype(BF16)
    ba = b_alpha[0].reshape(1, GK_W)
    gmix = g_mix[0].reshape(1, D_MODEL)
    gffn = g_ffn[0].reshape(1, D_MODEL)
    gq = jnp.tile(g_q[0], N_Q_HEADS).reshape(1, ATTN_WIDTH)
    gk = jnp.tile(g_k[0], N_KV_HEADS).reshape(1, KV_W)
    gout = g_gla_out[0].reshape(1, GLA_DV)
    wout = w_out[0].astype(BF16)
    wup = w_up[0].astype(BF16)
    wdown = w_down[0].astype(BF16)
    sk = sinks[0]
    lane = jnp.arange(LANES)
    ones_blk = (lane[:, None] // HEAD_DIM == lane[None, :] // HEAD_DIM).astype(BF16)

    h_p, k_p, v_p, st_p = _prompt_mixer(x_prompt, sk, gmix, win, wa, ba, gq, gk, ones_blk, gout, wout, tq=tq)
    y_p = _ffn(h_p.reshape(bsz * seq, D_MODEL), gffn, wup, wdown, tm=min(512, bsz * seq), ff_blk=512)

    rows = n_streams * t_new
    ck = cache_k[0].reshape(n_streams * WINDOW, KV_W)
    cv = cache_v[0].reshape(n_streams * WINDOW, KV_W)
    st0 = _state_to_pairs(state_gla[0])
    h_s, k_s, v_s, st_s = _sample_mixer(x_sample.reshape(rows, D_MODEL), ck, cv, st0, sk, gmix, win, wa, ba, gq, gk,
                                        ones_blk, gout, wout, n_streams=n_streams, t_new=t_new)
    y_s = _ffn(h_s, gffn, wup, wdown, tm=rows, ff_blk=512)

    kv_shape = (1, bsz, WINDOW, N_KV_HEADS, HEAD_DIM)
    new_shape = (1, n_streams, t_new, N_KV_HEADS, HEAD_DIM)
    return (y_p.reshape(bsz, seq, D_MODEL), y_s.reshape(n_streams, t_new, D_MODEL),
            k_p.reshape(kv_shape), v_p.reshape(kv_shape), _pairs_to_state(st_p)[None],
            k_s.reshape(new_shape), v_s.reshape(new_shape), _pairs_to_state(st_s)[None])
```
